```python
import jax, jax.numpy as jnp
from jax import lax
import numpy as np

D_MODEL = 2048
BATCH = 8
SEQ = 2048
DEPTH = 1
DEC_BATCH = 16
DEC_SEQ = 2048
PAST_LEN = 128

HEAD_DIM = 64
A_WIDTH = D_MODEL // 2
B_WIDTH = D_MODEL - A_WIDTH
N_HEADS = A_WIDTH // HEAD_DIM
CHUNK = 128
SGU_GROUPS = 8
SGU_CH = B_WIDTH // SGU_GROUPS
PATTERNS = ((128, 1), (512, 4), (2048, 16))
N_BUCKETS = 32
MAX_DIST = 1024
PROJ_WIDTH = 4 * A_WIDTH + 3 * B_WIDTH
RMS_EPS = 1e-6
LN_EPS = 1e-5
NEG = -1e30

kernel_name = "hybrid_dilated_attn_sgu_encoder"


def _rmsnorm(x, w):
    xf = x.astype(jnp.float32)
    xf = xf * lax.rsqrt(jnp.mean(xf * xf, axis=-1, keepdims=True) + RMS_EPS)
    return (xf * w.astype(jnp.float32)).astype(x.dtype)


def _t5_buckets(rel):
    nbk = N_BUCKETS // 2
    ret = (rel > 0).astype(np.int32) * nbk
    n = np.abs(rel)
    max_exact = nbk // 2
    large = max_exact + (np.log(np.maximum(n, 1) / max_exact)
                         / np.log(MAX_DIST / max_exact) * (nbk - max_exact)).astype(np.int32)
    large = np.minimum(large, nbk - 1)
    return (ret + np.where(n < max_exact, n, large)).astype(np.int32)


def _dilated_window_attn(q, k, v, rel_bias, dilation, half):
    Bq, S, H, Dh = q.shape
    L = S // dilation
    nb = -(-L // half)
    Lp = nb * half

    def residue_major(t):
        return t.reshape(Bq, L, dilation, H, Dh).transpose(0, 2, 1, 3, 4)

    qs = jnp.pad(residue_major(q), ((0, 0), (0, 0), (0, Lp - L), (0, 0), (0, 0)))
    pad_kv = ((0, 0), (0, 0), (half, Lp - L + half), (0, 0), (0, 0))
    ks = jnp.pad(residue_major(k), pad_kv)
    vs = jnp.pad(residue_major(v), pad_kv)
    qb = qs.reshape(Bq, dilation, nb, half, H, Dh)

    def band(t):
        tb = t.reshape(Bq, dilation, nb + 2, half, H, Dh)
        return jnp.concatenate([tb[:, :, :-2], tb[:, :, 1:-1], tb[:, :, 2:]], axis=3)

    kb, vb = band(ks), band(vs)

    qi = np.arange(half)[:, None]
    kj = np.arange(3 * half)[None, :]
    rel = kj - half - qi
    key_pos = np.arange(nb)[:, None, None] * half + (kj - half)[None]
    valid = (np.abs(rel) <= half)[None] & (key_pos >= 0) & (key_pos < L)
    bias = rel_bias[_t5_buckets(rel * dilation)]
    bias = jnp.transpose(bias, (2, 0, 1)).astype(jnp.float32)

    s = jnp.einsum('brnqhd,brnkhd->brnhqk', qb, kb) + bias[None, None, None]
    s = jnp.where(valid[None, None, :, None], s, NEG)
    m = jnp.max(s, axis=-1)
    p = jnp.exp(s - m[..., None])
    den = jnp.sum(p, axis=-1)
    num = jnp.einsum('brnhqk,brnkhd->brnqhd', p, vb)

    def back_stat(t):
        t = jnp.transpose(t, (0, 1, 2, 4, 3)).reshape(Bq, dilation, Lp, H)[:, :, :L]
        return jnp.transpose(t, (0, 2, 1, 3)).reshape(Bq, S, H)

    num = num.reshape(Bq, dilation, Lp, H, Dh)[:, :, :L]
    num = jnp.transpose(num, (0, 2, 1, 3, 4)).reshape(Bq, S, H, Dh)
    return num, back_stat(den), back_stat(m)


def _mixer_layer(x, rel_bias, norm_w, w_in, q_norm_w, k_norm_w,
                 sgu_ln_w, sgu_ln_b, w_spatial, b_spatial, w_out):
    Bx, S, _ = x.shape
    h = _rmsnorm(x, norm_w)
    proj = jnp.einsum('bsd,de->bse', h, w_in)
    cuts = [A_WIDTH, 2 * A_WIDTH, 3 * A_WIDTH, 4 * A_WIDTH,
            4 * A_WIDTH + B_WIDTH, 4 * A_WIDTH + 2 * B_WIDTH]
    q, k, v, g_a, u, vv, g_b = jnp.split(proj, cuts, axis=-1)

    def qk_norm(t, w):
        t = t.reshape(Bx, S, N_HEADS, HEAD_DIM).astype(jnp.float32)
        t = t * lax.rsqrt(jnp.mean(t * t, axis=-1, keepdims=True) + RMS_EPS)
        return t * w.astype(jnp.float32)

    qh = qk_norm(q, q_norm_w) * (HEAD_DIM ** -0.5)
    kh = qk_norm(k, k_norm_w)
    vh = v.reshape(Bx, S, N_HEADS, HEAD_DIM).astype(jnp.float32)
    nums, dens, maxs = [], [], []
    for window, dilation in PATTERNS:
        half = window // (2 * dilation)
        n_, d_, m_ = _dilated_window_attn(qh, kh, vh, rel_bias, dilation, half)
        nums.append(n_); dens.append(d_); maxs.append(m_)
    m_all = jnp.stack(maxs)
    wgt = jnp.exp(m_all - jnp.max(m_all, axis=0, keepdims=True))
    num_tot = jnp.sum(jnp.stack(nums) * wgt[..., None], axis=0)
    den_tot = jnp.sum(jnp.stack(dens) * wgt, axis=0)
    attn = (num_tot / den_tot[..., None]).reshape(Bx, S, A_WIDTH).astype(x.dtype)
    a_out = attn * jax.nn.silu(g_a)

    u = jax.nn.gelu(u)
    vf = jax.nn.gelu(vv).astype(jnp.float32)
    mu = jnp.mean(vf, axis=-1, keepdims=True)
    var = jnp.mean(jnp.square(vf - mu), axis=-1, keepdims=True)
    vf = (vf - mu) * lax.rsqrt(var + LN_EPS) * sgu_ln_w.astype(jnp.float32) + sgu_ln_b.astype(jnp.float32)
    vc = vf.astype(x.dtype).reshape(Bx, S // CHUNK, CHUNK, SGU_GROUPS, SGU_CH)
    mixed = jnp.einsum('gij,bnjgc->bnigc', w_spatial, vc) + jnp.transpose(b_spatial)[None, None, :, :, None]
    b_out = u * mixed.reshape(Bx, S, B_WIDTH) * jax.nn.silu(g_b)

    out = jnp.einsum('bse,ed->bsd', jnp.concatenate([a_out, b_out], axis=-1), w_out)
    return x + out


def setup_inputs(seed: int = 0) -> dict:
    key = jax.random.key(seed)
    ks = jax.random.split(key, 12)
    f32 = jnp.float32
    return {
        "x_prompt": jax.random.normal(ks[0], (BATCH, SEQ, D_MODEL), f32),
        "x_sample": jax.random.normal(ks[1], (DEC_BATCH, DEC_SEQ, D_MODEL), f32),
        "rel_bias": 0.1 * jax.random.normal(ks[2], (N_BUCKETS, N_HEADS), f32),
        "norm_w": 1.0 + 0.02 * jax.random.normal(ks[3], (DEPTH, D_MODEL), f32),
        "w_in": jax.random.normal(ks[4], (DEPTH, D_MODEL, PROJ_WIDTH), f32) * D_MODEL ** -0.5,
        "q_norm_w": 1.0 + 0.02 * jax.random.normal(ks[5], (DEPTH, HEAD_DIM), f32),
        "k_norm_w": 1.0 + 0.02 * jax.random.normal(ks[6], (DEPTH, HEAD_DIM), f32),
        "sgu_ln_w": 1.0 + 0.02 * jax.random.normal(ks[7], (DEPTH, B_WIDTH), f32),
        "sgu_ln_b": 0.02 * jax.random.normal(ks[8], (DEPTH, B_WIDTH), f32),
        "w_spatial": jax.random.normal(ks[9], (DEPTH, SGU_GROUPS, CHUNK, CHUNK), f32) * CHUNK ** -0.5,
        "b_spatial": 1.0 + 0.1 * jax.random.normal(ks[10], (DEPTH, SGU_GROUPS, CHUNK), f32),
        "w_out": jax.random.normal(ks[11], (DEPTH, D_MODEL, D_MODEL), f32) * D_MODEL ** -0.5,
    }


def reference(x_prompt, x_sample, rel_bias, norm_w, w_in, q_norm_w, k_norm_w,
              sgu_ln_w, sgu_ln_b, w_spatial, b_spatial, w_out):
    y_prompt = x_prompt
    y_sample = x_sample
    for layer in range(DEPTH):
        params = (norm_w[layer], w_in[layer], q_norm_w[layer], k_norm_w[layer],
                  sgu_ln_w[layer], sgu_ln_b[layer], w_spatial[layer], b_spatial[layer],
                  w_out[layer])
        y_prompt = _mixer_layer(y_prompt, rel_bias, *params)
        y_sample = _mixer_layer(y_sample, rel_bias, *params)
    return (y_prompt, y_sample)
```

```python
import functools

import numpy as np
import jax
import jax.numpy as jnp
from jax import lax
from jax.experimental import pallas as pl
from jax.experimental.pallas import tpu as pltpu

D_MODEL = 2048
HEAD_DIM = 64
A_WIDTH = D_MODEL // 2
B_WIDTH = D_MODEL - A_WIDTH
N_HEADS = A_WIDTH // HEAD_DIM
CHUNK = 128
SGU_GROUPS = 8
SGU_CH = B_WIDTH // SGU_GROUPS
DILATIONS = (1, 4, 16)
HALF = 64
N_BUCKETS = 32
MAX_DIST = 1024
N_SEG = 7
SEG_W = 1024
PROJ_WIDTH = N_SEG * SEG_W
RMS_EPS = 1e-6
LN_EPS = 1e-5
NEG = -1e30

LANES = 128
QB = 128
KB = 256
PERM_ROWS = 256
TM_PROJ = 1024
TM_OUT = 512
VMEM_LIMIT = 56 * 1024 * 1024

F32 = jnp.float32
BF16 = jnp.bfloat16


def _t5_buckets(rel):
    nbk = N_BUCKETS // 2
    ret = (rel > 0).astype(np.int32) * nbk
    n = np.abs(rel)
    max_exact = nbk // 2
    large = max_exact + (np.log(np.maximum(n, 1) / max_exact)
                         / np.log(MAX_DIST / max_exact) * (nbk - max_exact)).astype(np.int32)
    large = np.minimum(large, nbk - 1)
    return (ret + np.where(n < max_exact, n, large)).astype(np.int32)


def _perm_matrix(dilation):
    per = PERM_ROWS // dilation
    p = np.zeros((PERM_ROWS, PERM_ROWS), np.float32)
    for r in range(dilation):
        for l in range(per):
            p[r * per + l, dilation * l + r] = 1.0
    return p


def _bias_tables(rel_bias, dilation, width, shifts):
    qi = np.arange(QB)[:, None]
    kj = np.arange(width)[None, :]
    tabs = []
    for c in shifts:
        rel = kj - c - qi
        valid = np.abs(rel) <= HALF
        bk = _t5_buckets(rel * dilation)
        b = rel_bias[bk]
        b = jnp.where(valid[:, :, None], b.astype(F32), NEG)
        b = jnp.transpose(b, (2, 0, 1))
        tabs.append(b.reshape(N_HEADS // 2, 2 * QB, width))
    return jnp.stack(tabs)


def _head_sumsq(t):
    lane_i = lax.broadcasted_iota(jnp.int32, (LANES, LANES), 0) // HEAD_DIM
    lane_j = lax.broadcasted_iota(jnp.int32, (LANES, LANES), 1) // HEAD_DIM
    ones_bd = jnp.where(lane_i == lane_j, 1.0, 0.0).astype(BF16)
    t2 = t * t
    hi = t2.astype(BF16)
    lo = (t2 - hi.astype(F32)).astype(BF16)
    parts = []
    for c in range(SEG_W // LANES):
        sl = slice(c * LANES, (c + 1) * LANES)
        parts.append(jnp.dot(hi[:, sl], ones_bd, preferred_element_type=F32)
                     + jnp.dot(lo[:, sl], ones_bd, preferred_element_type=F32))
    return jnp.concatenate(parts, axis=1)


def _proj_kernel(x_ref, nw_ref, w_ref, qw_ref, kw_ref, lnw_ref, lnb_ref, p4_ref, p16_ref,
                 nat_ref, o4_ref, o16_ref, h_ref, t_ref):
    j = pl.program_id(2)

    @pl.when(j == 0)
    def _():
        x = x_ref[...]
        ms = jnp.mean(x * x, axis=-1, keepdims=True)
        h_ref[...] = (x * lax.rsqrt(ms + RMS_EPS) * nw_ref[...]).astype(BF16)

    t_ref[...] = jnp.dot(h_ref[...], w_ref[...], preferred_element_type=F32)

    n_sub = TM_PROJ // PERM_ROWS

    def write_perms(c, val):
        for d, p_ref, o_ref in ((4, p4_ref, o4_ref), (16, p16_ref, o16_ref)):
            per = PERM_ROWS // d
            pv = jnp.dot(p_ref[...], val, preferred_element_type=F32).astype(BF16)
            for r in range(d):
                o_ref[r, c * per:(c + 1) * per, :] = pv[r * per:(r + 1) * per, :]

    def qk_norm(t, w, scale):
        ss = _head_sumsq(t)
        tn = t * lax.rsqrt(ss * (1.0 / HEAD_DIM) + RMS_EPS) * w
        if scale != 1.0:
            tn = tn * scale
        return tn

    def for_subblocks(fn):
        for c in range(n_sub):
            rows = slice(c * PERM_ROWS, (c + 1) * PERM_ROWS)
            fn(c, rows, t_ref[rows, :])

    @pl.when(j == 0)
    def _():
        def fn(c, rows, t):
            val = qk_norm(t, qw_ref[...], HEAD_DIM ** -0.5).astype(BF16)
            nat_ref[rows, :] = val
            write_perms(c, val)
        for_subblocks(fn)

    @pl.when(j == 1)
    def _():
        def fn(c, rows, t):
            val = qk_norm(t, kw_ref[...], 1.0).astype(BF16)
            nat_ref[rows, :] = val
            write_perms(c, val)
        for_subblocks(fn)

    @pl.when(j == 2)
    def _():
        def fn(c, rows, t):
            val = t.astype(BF16)
            nat_ref[rows, :] = val
            write_perms(c, val)
        for_subblocks(fn)

    @pl.when((j == 3) | (j == 6))
    def _():
        def fn(c, rows, t):
            nat_ref[rows, :] = jax.nn.silu(t).astype(BF16)
        for_subblocks(fn)

    @pl.when(j == 4)
    def _():
        def fn(c, rows, t):
            nat_ref[rows, :] = jax.nn.gelu(t).astype(BF16)
        for_subblocks(fn)

    @pl.when(j == 5)
    def _():
        def fn(c, rows, t):
            g = jax.nn.gelu(t)
            mu = jnp.mean(g, axis=-1, keepdims=True)
            gc = g - mu
            var = jnp.mean(gc * gc, axis=-1, keepdims=True)
            nat_ref[rows, :] = (gc * lax.rsqrt(var + LN_EPS) * lnw_ref[...]
                                + lnb_ref[...]).astype(BF16)
        for_subblocks(fn)


def _proj_call(x, norm_w, w_in, q_w, k_w, ln_w, ln_b, p4, p16):
    bsz, seq, _ = x.shape
    n_t = seq // TM_PROJ
    const2 = lambda b, t, j: (0, 0)
    return pl.pallas_call(
        _proj_kernel,
        grid=(bsz, n_t, N_SEG),
        in_specs=[
            pl.BlockSpec((None, TM_PROJ, D_MODEL), lambda b, t, j: (b, t, 0)),
            pl.BlockSpec((1, D_MODEL), const2),
            pl.BlockSpec((D_MODEL, SEG_W), lambda b, t, j: (0, j)),
            pl.BlockSpec((1, SEG_W), const2),
            pl.BlockSpec((1, SEG_W), const2),
            pl.BlockSpec((1, SEG_W), const2),
            pl.BlockSpec((1, SEG_W), const2),
            pl.BlockSpec((PERM_ROWS, PERM_ROWS), const2),
            pl.BlockSpec((PERM_ROWS, PERM_ROWS), const2),
        ],
        out_specs=[
            pl.BlockSpec((None, TM_PROJ, SEG_W), lambda b, t, j: (b, t, j)),
            pl.BlockSpec((None, 4, TM_PROJ // 4, SEG_W),
                         lambda b, t, j: (b, 0, t, jnp.minimum(j, 2))),
            pl.BlockSpec((None, 16, TM_PROJ // 16, SEG_W),
                         lambda b, t, j: (b, 0, t, jnp.minimum(j, 2))),
        ],
        out_shape=[
            jax.ShapeDtypeStruct((bsz, seq, PROJ_WIDTH), BF16),
            jax.ShapeDtypeStruct((bsz, 4, seq // 4, 3 * SEG_W), BF16),
            jax.ShapeDtypeStruct((bsz, 16, seq // 16, 3 * SEG_W), BF16),
        ],
        scratch_shapes=[
            pltpu.VMEM((TM_PROJ, D_MODEL), BF16),
            pltpu.VMEM((TM_PROJ, SEG_W), F32),
        ],
        compiler_params=pltpu.CompilerParams(
            dimension_semantics=("arbitrary", "arbitrary", "arbitrary"),
            vmem_limit_bytes=VMEM_LIMIT),
        name="proj",
    )(x, norm_w, w_in, q_w, k_w, ln_w, ln_b, p4, p16)


def _attn_block(qb, kw, vw, tab):
    lane = lax.broadcasted_iota(jnp.int32, (QB, LANES), 1)
    lo = lane < HEAD_DIM
    zero = jnp.zeros_like(qb)
    qs = jnp.concatenate([jnp.where(lo, qb, zero), jnp.where(lo, zero, qb)], axis=0)
    s = lax.dot_general(qs, kw, (((1,), (1,)), ((), ())), preferred_element_type=F32) + tab
    m = jnp.max(s, axis=-1, keepdims=True)
    p = jnp.exp(s - m).astype(BF16)
    vext = jnp.concatenate([vw, jnp.ones_like(vw)], axis=1)
    o = jnp.dot(p, vext, preferred_element_type=F32)
    num = jnp.where(lo, o[:QB, :LANES], o[QB:, :LANES])
    den = jnp.where(lo, o[:QB, LANES:], o[QB:, LANES:])
    mm = jnp.where(lo, jnp.broadcast_to(m[:QB], (QB, LANES)),
                   jnp.broadcast_to(m[QB:], (QB, LANES)))
    return num, den, mm


def _attn_kernel(q1_ref, k1_ref, v1_ref, q4_ref, k4_ref, v4_ref, q16_ref, k16_ref, v16_ref,
                 g_ref, tab1_ref, tab4_ref, tab16_ref, o_ref, num_ref, den_ref, max_ref):
    seq = q1_ref.shape[0]

    def store(pat, start, stride, vals):
        num, den, mm = vals
        if stride == 1:
            idx = pl.ds(start, QB)
        else:
            idx = pl.ds(start, QB, stride=stride)
        num_ref[pat, idx, :] = num
        den_ref[pat, idx, :] = den
        max_ref[pat, idx, :] = mm

    def run_subsequence(pat, dil, q_ref, k_ref, v_ref, tab_ref, length, row0):
        nb = length // QB
        if nb == 1:
            store(pat, row0, dil, _attn_block(q_ref[...], k_ref[...], v_ref[...], tab_ref[...]))
            return
        store(pat, row0, dil,
              _attn_block(q_ref[0:QB, :], k_ref[0:KB, :], v_ref[0:KB, :], tab_ref[0]))

        def body(n, carry):
            q0 = pl.multiple_of(n * QB, QB)
            k0 = pl.multiple_of(n * QB - HALF, HALF)
            vals = _attn_block(q_ref[pl.ds(q0, QB), :], k_ref[pl.ds(k0, KB), :],
                               v_ref[pl.ds(k0, KB), :], tab_ref[1])
            store(pat, row0 + q0 * dil, dil, vals)
            return carry

        lax.fori_loop(1, nb - 1, body, 0)
        store(pat, row0 + (length - QB) * dil, dil,
              _attn_block(q_ref[length - QB:length, :], k_ref[length - KB:length, :],
                          v_ref[length - KB:length, :], tab_ref[2]))

    run_subsequence(0, 1, q1_ref, k1_ref, v1_ref, tab1_ref, seq, 0)

    def body4(r, carry):
        run_subsequence(1, 4, q4_ref.at[r], k4_ref.at[r], v4_ref.at[r], tab4_ref, seq // 4, r)
        return carry
    lax.fori_loop(0, 4, body4, 0)

    def body16(r, carry):
        run_subsequence(2, 16, q16_ref.at[r], k16_ref.at[r], v16_ref.at[r], tab16_ref,
                        seq // 16, r)
        return carry
    lax.fori_loop(0, 16, body16, 0)

    rows_per = 256

    def merge(i, carry):
        r0 = pl.multiple_of(i * rows_per, rows_per)
        sl = pl.ds(r0, rows_per)
        m0, m1, m2 = max_ref[0, sl, :], max_ref[1, sl, :], max_ref[2, sl, :]
        mx = jnp.maximum(jnp.maximum(m0, m1), m2)
        w0, w1, w2 = jnp.exp(m0 - mx), jnp.exp(m1 - mx), jnp.exp(m2 - mx)
        num = num_ref[0, sl, :] * w0 + num_ref[1, sl, :] * w1 + num_ref[2, sl, :] * w2
        den = den_ref[0, sl, :] * w0 + den_ref[1, sl, :] * w1 + den_ref[2, sl, :] * w2
        o_ref[sl, :] = ((num / den) * g_ref[sl, :].astype(F32)).astype(BF16)
        return carry

    lax.fori_loop(0, seq // rows_per, merge, 0)


def _attn_call(nat, r4, r16, tab1, tab4, tab16):
    bsz, seq, _ = nat.shape
    n_hp = N_HEADS // 2
    cols = SEG_W // LANES

    def nat_spec(seg):
        return pl.BlockSpec((None, seq, LANES), lambda hp, b: (b, 0, seg * cols + hp))

    def res_spec(d, seg):
        return pl.BlockSpec((None, d, seq // d, LANES), lambda hp, b: (b, 0, 0, seg * cols + hp))

    def tab_spec(shape):
        nd = len(shape)
        if nd == 4:
            return pl.BlockSpec((shape[0], None, shape[2], shape[3]), lambda hp, b: (0, hp, 0, 0))
        return pl.BlockSpec((None, shape[1], shape[2]), lambda hp, b: (hp, 0, 0))

    return pl.pallas_call(
        _attn_kernel,
        grid=(n_hp, bsz),
        in_specs=[nat_spec(0), nat_spec(1), nat_spec(2),
                  res_spec(4, 0), res_spec(4, 1), res_spec(4, 2),
                  res_spec(16, 0), res_spec(16, 1), res_spec(16, 2),
                  nat_spec(3),
                  tab_spec(tab1.shape), tab_spec(tab4.shape), tab_spec(tab16.shape)],
        out_specs=pl.BlockSpec((None, seq, LANES), lambda hp, b: (b, 0, hp)),
        out_shape=jax.ShapeDtypeStruct((bsz, seq, A_WIDTH), BF16),
        scratch_shapes=[pltpu.VMEM((3, seq, LANES), F32),
                        pltpu.VMEM((3, seq, LANES), F32),
                        pltpu.VMEM((3, seq, LANES), F32)],
        compiler_params=pltpu.CompilerParams(
            dimension_semantics=("arbitrary", "arbitrary"),
            vmem_limit_bytes=VMEM_LIMIT),
        name="attn",
    )(nat, nat, nat, r4, r4, r4, r16, r16, r16, nat, tab1, tab4, tab16)


def _out_kernel(x_ref, a_ref, u_ref, v_ref, g_ref, wsp_ref, bsp_ref, wo_ref, y_ref, b_scr):
    for n in range(TM_OUT // CHUNK):
        rows = slice(n * CHUNK, (n + 1) * CHUNK)
        for g in range(SGU_GROUPS):
            cols = slice(g * SGU_CH, (g + 1) * SGU_CH)
            mixed = jnp.dot(wsp_ref[g], v_ref[rows, cols], preferred_element_type=F32) + bsp_ref[g]
            b_scr[rows, cols] = (u_ref[rows, cols].astype(F32) * mixed
                                 * g_ref[rows, cols].astype(F32)).astype(BF16)
    out = jnp.dot(a_ref[...], wo_ref[0:A_WIDTH, :], preferred_element_type=F32)
    out = out + jnp.dot(b_scr[...], wo_ref[A_WIDTH:, :], preferred_element_type=F32)
    y_ref[...] = x_ref[...] + out


def _out_call(x, a_out, nat, w_sp, b_sp, w_out):
    bsz, seq, _ = x.shape
    n_t = seq // TM_OUT

    def nat_spec(seg):
        return pl.BlockSpec((None, TM_OUT, SEG_W), lambda b, t: (b, t, seg))

    return pl.pallas_call(
        _out_kernel,
        grid=(bsz, n_t),
        in_specs=[
            pl.BlockSpec((None, TM_OUT, D_MODEL), lambda b, t: (b, t, 0)),
            pl.BlockSpec((None, TM_OUT, A_WIDTH), lambda b, t: (b, t, 0)),
            nat_spec(4), nat_spec(5), nat_spec(6),
            pl.BlockSpec((SGU_GROUPS, CHUNK, CHUNK), lambda b, t: (0, 0, 0)),
            pl.BlockSpec((SGU_GROUPS, CHUNK, SGU_CH), lambda b, t: (0, 0, 0)),
            pl.BlockSpec((D_MODEL, D_MODEL), lambda b, t: (0, 0)),
        ],
        out_specs=pl.BlockSpec((None, TM_OUT, D_MODEL), lambda b, t: (b, t, 0)),
        out_shape=jax.ShapeDtypeStruct((bsz, seq, D_MODEL), F32),
        scratch_shapes=[pltpu.VMEM((TM_OUT, B_WIDTH), BF16)],
        compiler_params=pltpu.CompilerParams(
            dimension_semantics=("arbitrary", "arbitrary"),
            vmem_limit_bytes=VMEM_LIMIT),
        name="out",
    )(x, a_out, nat, nat, nat, w_sp, b_sp, w_out)


def _layer(x, params):
    (norm_w, w_in, q_w, k_w, ln_w, ln_b, p4, p16, tab1, tab4, tab16, w_sp, b_sp, w_out) = params
    nat, r4, r16 = _proj_call(x, norm_w, w_in, q_w, k_w, ln_w, ln_b, p4, p16)
    a_out = _attn_call(nat, r4, r16, tab1, tab4, tab16)
    return _out_call(x, a_out, nat, w_sp, b_sp, w_out)


def kernel(x_prompt, x_sample, rel_bias, norm_w, w_in, q_norm_w, k_norm_w,
           sgu_ln_w, sgu_ln_b, w_spatial, b_spatial, w_out):
    depth = norm_w.shape[0]
    p4 = jnp.asarray(_perm_matrix(4), BF16)
    p16 = jnp.asarray(_perm_matrix(16), BF16)
    tab1 = _bias_tables(rel_bias, 1, KB, (0, HALF, 2 * HALF))
    tab4 = _bias_tables(rel_bias, 4, KB, (0, HALF, 2 * HALF))
    tab16 = _bias_tables(rel_bias, 16, QB, (0,))[0]
    y_prompt, y_sample = x_prompt, x_sample
    for layer in range(depth):
        params = (
            norm_w[layer].reshape(1, D_MODEL),
            w_in[layer].astype(BF16),
            jnp.tile(q_norm_w[layer], N_HEADS).reshape(1, SEG_W),
            jnp.tile(k_norm_w[layer], N_HEADS).reshape(1, SEG_W),
            sgu_ln_w[layer].reshape(1, B_WIDTH),
            sgu_ln_b[layer].reshape(1, B_WIDTH),
            p4, p16, tab1, tab4, tab16,
            w_spatial[layer].astype(BF16),
            jnp.broadcast_to(b_spatial[layer][:, :, None], (SGU_GROUPS, CHUNK, SGU_CH)),
            w_out[layer].astype(BF16),
        )
        y_prompt = _layer(y_prompt, params)
        y_sample = _layer(y_sample, params)
    return (y_prompt, y_sample)
```

```python
import functools

import numpy as np
import jax
import jax.numpy as jnp
from jax import lax
from jax.experimental import pallas as pl
from jax.experimental.pallas import tpu as pltpu

D_MODEL = 2048
HEAD_DIM = 64
A_WIDTH = D_MODEL // 2
B_WIDTH = D_MODEL - A_WIDTH
N_HEADS = A_WIDTH // HEAD_DIM
CHUNK = 128
SGU_GROUPS = 8
SGU_CH = B_WIDTH // SGU_GROUPS
DILATIONS = (1, 4, 16)
HALF = 64
N_BUCKETS = 32
MAX_DIST = 1024
N_SEG = 7
SEG_W = 1024
PROJ_WIDTH = N_SEG * SEG_W
RMS_EPS = 1e-6
LN_EPS = 1e-5
NEG = -1e30
LOG2E = 1.4426950408889634

LANES = 128
QB = 128
KB = 256
GROUP = 4
PERM_ROWS = 256
TM_PROJ = 1024
TM_OUT = 512
VMEM_LIMIT = 56 * 1024 * 1024

F32 = jnp.float32
BF16 = jnp.bfloat16


def _t5_buckets(rel):
    nbk = N_BUCKETS // 2
    ret = (rel > 0).astype(np.int32) * nbk
    n = np.abs(rel)
    max_exact = nbk // 2
    large = max_exact + (np.log(np.maximum(n, 1) / max_exact)
                         / np.log(MAX_DIST / max_exact) * (nbk - max_exact)).astype(np.int32)
    large = np.minimum(large, nbk - 1)
    return (ret + np.where(n < max_exact, n, large)).astype(np.int32)


def _perm_matrix(dilation):
    per = PERM_ROWS // dilation
    p = np.zeros((PERM_ROWS, PERM_ROWS), np.float32)
    for r in range(dilation):
        for l in range(per):
            p[r * per + l, dilation * l + r] = 1.0
    return p


def _bias_tables(rel_bias, dilation, width, shifts):
    n = width + QB
    k = np.arange(n)
    diff = np.where(k < width, k, k - n)
    tabs = []
    for c in shifts:
        rel = diff - c
        valid = np.abs(rel) <= HALF
        bk = _t5_buckets(np.clip(rel, -HALF, HALF) * dilation)
        g = jnp.where(valid[:, None], rel_bias[bk].astype(F32) * LOG2E, NEG).T
        flat = jnp.tile(g, (1, QB))[:, :QB * (n - 1)]
        t = flat.reshape(N_HEADS, QB, n - 1)[:, :, :width]
        tabs.append(t.reshape(N_HEADS // 2, 2 * QB, width))
    return jnp.stack(tabs)


def _head_sumsq(t):
    lane_i = lax.broadcasted_iota(jnp.int32, (LANES, LANES), 0) // HEAD_DIM
    lane_j = lax.broadcasted_iota(jnp.int32, (LANES, LANES), 1) // HEAD_DIM
    ones_bd = jnp.where(lane_i == lane_j, 1.0, 0.0).astype(BF16)
    t2 = t * t
    hi = t2.astype(BF16)
    lo = (t2 - hi.astype(F32)).astype(BF16)
    parts = []
    for c in range(SEG_W // LANES):
        sl = slice(c * LANES, (c + 1) * LANES)
        parts.append(jnp.dot(hi[:, sl], ones_bd, preferred_element_type=F32)
                     + jnp.dot(lo[:, sl], ones_bd, preferred_element_type=F32))
    return jnp.concatenate(parts, axis=1)


def _proj_kernel(x_ref, nw_ref, w_ref, qw_ref, kw_ref, lnw_ref, lnb_ref, p4_ref, p16_ref,
                 nat_ref, o4_ref, o16_ref, h_ref, t_ref):
    j = pl.program_id(2)

    @pl.when(j == 0)
    def _():
        x = x_ref[...]
        ms = jnp.mean(x * x, axis=-1, keepdims=True)
        h_ref[...] = (x * lax.rsqrt(ms + RMS_EPS) * nw_ref[...]).astype(BF16)

    t_ref[...] = jnp.dot(h_ref[...], w_ref[...], preferred_element_type=F32)

    n_sub = TM_PROJ // PERM_ROWS

    def write_perms(c, val):
        for d, p_ref, o_ref in ((4, p4_ref, o4_ref), (16, p16_ref, o16_ref)):
            per = PERM_ROWS // d
            pv = jnp.dot(p_ref[...], val, preferred_element_type=F32).astype(BF16)
            for r in range(d):
                o_ref[r, c * per:(c + 1) * per, :] = pv[r * per:(r + 1) * per, :]

    def qk_norm(t, w, scale):
        ss = _head_sumsq(t)
        tn = t * lax.rsqrt(ss * (1.0 / HEAD_DIM) + RMS_EPS) * w
        if scale != 1.0:
            tn = tn * scale
        return tn

    def for_subblocks(fn):
        for c in range(n_sub):
            rows = slice(c * PERM_ROWS, (c + 1) * PERM_ROWS)
            fn(c, rows, t_ref[rows, :])

    @pl.when(j == 0)
    def _():
        def fn(c, rows, t):
            val = qk_norm(t, qw_ref[...], HEAD_DIM ** -0.5 * LOG2E).astype(BF16)
            nat_ref[rows, :] = val
            write_perms(c, val)
        for_subblocks(fn)

    @pl.when(j == 1)
    def _():
        def fn(c, rows, t):
            val = qk_norm(t, kw_ref[...], 1.0).astype(BF16)
            nat_ref[rows, :] = val
            write_perms(c, val)
        for_subblocks(fn)

    @pl.when(j == 2)
    def _():
        def fn(c, rows, t):
            val = t.astype(BF16)
            nat_ref[rows, :] = val
            write_perms(c, val)
        for_subblocks(fn)

    @pl.when((j == 3) | (j == 6))
    def _():
        def fn(c, rows, t):
            nat_ref[rows, :] = jax.nn.silu(t).astype(BF16)
        for_subblocks(fn)

    @pl.when(j == 4)
    def _():
        def fn(c, rows, t):
            nat_ref[rows, :] = jax.nn.gelu(t).astype(BF16)
        for_subblocks(fn)

    @pl.when(j == 5)
    def _():
        def fn(c, rows, t):
            g = jax.nn.gelu(t)
            mu = jnp.mean(g, axis=-1, keepdims=True)
            gc = g - mu
            var = jnp.mean(gc * gc, axis=-1, keepdims=True)
            nat_ref[rows, :] = (gc * lax.rsqrt(var + LN_EPS) * lnw_ref[...]
                                + lnb_ref[...]).astype(BF16)
        for_subblocks(fn)


def _proj_call(x, norm_w, w_in, q_w, k_w, ln_w, ln_b, p4, p16):
    bsz, seq, _ = x.shape
    n_t = seq // TM_PROJ
    const2 = lambda b, t, j: (0, 0)
    return pl.pallas_call(
        _proj_kernel,
        grid=(bsz, n_t, N_SEG),
        in_specs=[
            pl.BlockSpec((None, TM_PROJ, D_MODEL), lambda b, t, j: (b, t, 0)),
            pl.BlockSpec((1, D_MODEL), const2),
            pl.BlockSpec((D_MODEL, SEG_W), lambda b, t, j: (0, j)),
            pl.BlockSpec((1, SEG_W), const2),
            pl.BlockSpec((1, SEG_W), const2),
            pl.BlockSpec((1, SEG_W), const2),
            pl.BlockSpec((1, SEG_W), const2),
            pl.BlockSpec((PERM_ROWS, PERM_ROWS), const2),
            pl.BlockSpec((PERM_ROWS, PERM_ROWS), const2),
        ],
        out_specs=[
            pl.BlockSpec((None, TM_PROJ, SEG_W), lambda b, t, j: (b, t, j)),
            pl.BlockSpec((None, 4, TM_PROJ // 4, SEG_W),
                         lambda b, t, j: (b, 0, t, jnp.minimum(j, 2))),
            pl.BlockSpec((None, 16, TM_PROJ // 16, SEG_W),
                         lambda b, t, j: (b, 0, t, jnp.minimum(j, 2))),
        ],
        out_shape=[
            jax.ShapeDtypeStruct((bsz, seq, PROJ_WIDTH), BF16),
            jax.ShapeDtypeStruct((bsz, 4, seq // 4, 3 * SEG_W), BF16),
            jax.ShapeDtypeStruct((bsz, 16, seq // 16, 3 * SEG_W), BF16),
        ],
        scratch_shapes=[
            pltpu.VMEM((TM_PROJ, D_MODEL), BF16),
            pltpu.VMEM((TM_PROJ, SEG_W), F32),
        ],
        compiler_params=pltpu.CompilerParams(
            dimension_semantics=("arbitrary", "arbitrary", "arbitrary"),
            vmem_limit_bytes=VMEM_LIMIT),
        name="proj",
    )(x, norm_w, w_in, q_w, k_w, ln_w, ln_b, p4, p16)


def _head0_lanes():
    return lax.broadcasted_iota(jnp.int32, (QB, LANES), 1) < HEAD_DIM


def _probabilities(qb, kw, tab):
    lo = _head0_lanes()
    zero = jnp.zeros_like(qb)
    qs = jnp.concatenate([jnp.where(lo, qb, zero), jnp.where(lo, zero, qb)], axis=0)
    s = lax.dot_general(qs, kw, (((1,), (1,)), ((), ())), preferred_element_type=F32) + tab
    m = jnp.max(s, axis=-1, keepdims=True)
    p = jnp.exp2(s - m).astype(BF16)
    mm = jnp.where(lo, jnp.broadcast_to(m[:QB], (QB, LANES)),
                   jnp.broadcast_to(m[QB:], (QB, LANES)))
    return p, mm


def _weighted_values(p, vw):
    lo = _head0_lanes()
    vext = jnp.concatenate([vw, jnp.ones_like(vw)], axis=1)
    o = jnp.dot(p, vext, preferred_element_type=F32)
    num = jnp.where(lo, o[:QB, :LANES], o[QB:, :LANES])
    den = jnp.where(lo, o[:QB, LANES:], o[QB:, LANES:])
    return num, den


def _attn_kernel(q1_ref, k1_ref, v1_ref, q4_ref, k4_ref, v4_ref, q16_ref, k16_ref, v16_ref,
                 g_ref, tab1_ref, tab4_ref, tab16_ref, o_ref, out_scr, lse_scr, p_scr, m_scr):
    seq = q1_ref.shape[0]
    nb1, nb4 = seq // QB, seq // (4 * QB)

    def refs(dil, r):
        if dil == 1:
            return q1_ref, k1_ref, v1_ref, tab1_ref, nb1
        if dil == 4:
            return q4_ref.at[r], k4_ref.at[r], v4_ref.at[r], tab4_ref, nb4
        return q16_ref.at[r], k16_ref.at[r], v16_ref.at[r], tab16_ref, 1

    def key_rows(n, nb):
        if nb == 1:
            return slice(0, QB)
        k0 = min(max(n * QB - HALF, 0), nb * QB - KB)
        return slice(k0, k0 + KB)

    def first_stage(block, slot, g):
        dil, r, n = block
        q_ref, k_ref, _, tab_ref, nb = refs(dil, r)
        if nb == 1:
            tab = tab_ref[...]
        else:
            tab = tab_ref[0 if n == 0 else (2 if n == nb - 1 else 1)]
        p, mm = _probabilities(q_ref[n * QB:(n + 1) * QB, :], k_ref[key_rows(n, nb), :], tab)
        p_scr[slot, g, :, 0:p.shape[1]] = p
        m_scr[slot, g] = mm

    def second_stage(block, slot, g):
        dil, r, n = block
        _, _, v_ref, _, nb = refs(dil, r)
        vw = v_ref[key_rows(n, nb), :]
        num, den = _weighted_values(p_scr[slot, g, :, 0:vw.shape[0]], vw)
        mm = m_scr[slot, g]
        if dil != 1:
            idx = pl.ds(n * QB * dil + r, QB, stride=dil)
            slab = 0 if dil == 4 else 1
            out_scr[slab, idx, :] = num / den
            lse_scr[slab, idx, :] = mm + jnp.log2(den)
        else:
            rows = slice(n * QB, (n + 1) * QB)
            l4, l16 = lse_scr[0, rows, :], lse_scr[1, rows, :]
            mx = jnp.maximum(jnp.maximum(mm, l4), l16)
            w1, w4, w16 = jnp.exp2(mm - mx), jnp.exp2(l4 - mx), jnp.exp2(l16 - mx)
            numer = num * w1 + out_scr[0, rows, :] * w4 + out_scr[1, rows, :] * w16
            denom = den * w1 + w4 + w16
            o_ref[rows, :] = ((numer / denom) * g_ref[rows, :].astype(F32)).astype(BF16)

    groups = [[(16, 4 * i + g, 0) for g in range(GROUP)] for i in range(16 // GROUP)]
    for n4 in range(nb4):
        groups.append([(4, r, n4) for r in range(4)])
        groups.append([(1, 0, 4 * n4 + g) for g in range(GROUP)])

    for t in range(len(groups) + 1):
        for g in range(GROUP):
            if t >= 1:
                second_stage(groups[t - 1][g], (t - 1) % 2, g)
            if t < len(groups):
                first_stage(groups[t][g], t % 2, g)


def _attn_call(nat, r4, r16, tab1, tab4, tab16):
    bsz, seq, _ = nat.shape
    n_hp = N_HEADS // 2
    cols = SEG_W // LANES

    def nat_spec(seg):
        return pl.BlockSpec((None, seq, LANES), lambda hp, b: (b, 0, seg * cols + hp))

    def res_spec(d, seg):
        return pl.BlockSpec((None, d, seq // d, LANES), lambda hp, b: (b, 0, 0, seg * cols + hp))

    def tab_spec(shape):
        nd = len(shape)
        if nd == 4:
            return pl.BlockSpec((shape[0], None, shape[2], shape[3]), lambda hp, b: (0, hp, 0, 0))
        return pl.BlockSpec((None, shape[1], shape[2]), lambda hp, b: (hp, 0, 0))

    return pl.pallas_call(
        _attn_kernel,
        grid=(n_hp, bsz),
        in_specs=[nat_spec(0), nat_spec(1), nat_spec(2),
                  res_spec(4, 0), res_spec(4, 1), res_spec(4, 2),
                  res_spec(16, 0), res_spec(16, 1), res_spec(16, 2),
                  nat_spec(3),
                  tab_spec(tab1.shape), tab_spec(tab4.shape), tab_spec(tab16.shape)],
        out_specs=pl.BlockSpec((None, seq, LANES), lambda hp, b: (b, 0, hp)),
        out_shape=jax.ShapeDtypeStruct((bsz, seq, A_WIDTH), BF16),
        scratch_shapes=[pltpu.VMEM((2, seq, LANES), F32),
                        pltpu.VMEM((2, seq, LANES), F32),
                        pltpu.VMEM((2, GROUP, 2 * QB, KB), BF16),
                        pltpu.VMEM((2, GROUP, QB, LANES), F32)],
        compiler_params=pltpu.CompilerParams(
            dimension_semantics=("arbitrary", "arbitrary"),
            vmem_limit_bytes=VMEM_LIMIT),
        name="attn",
    )(nat, nat, nat, r4, r4, r4, r16, r16, r16, nat, tab1, tab4, tab16)


def _out_kernel(x_ref, a_ref, u_ref, v_ref, g_ref, wsp_ref, bsp_ref, wo_ref, y_ref, b_scr):
    for n in range(TM_OUT // CHUNK):
        rows = slice(n * CHUNK, (n + 1) * CHUNK)
        for g in range(SGU_GROUPS):
            cols = slice(g * SGU_CH, (g + 1) * SGU_CH)
            mixed = jnp.dot(wsp_ref[g], v_ref[rows, cols], preferred_element_type=F32) + bsp_ref[g]
            b_scr[rows, cols] = (u_ref[rows, cols].astype(F32) * mixed
                                 * g_ref[rows, cols].astype(F32)).astype(BF16)
    out = jnp.dot(a_ref[...], wo_ref[0:A_WIDTH, :], preferred_element_type=F32)
    out = out + jnp.dot(b_scr[...], wo_ref[A_WIDTH:, :], preferred_element_type=F32)
    y_ref[...] = x_ref[...] + out


def _out_call(x, a_out, nat, w_sp, b_sp, w_out):
    bsz, seq, _ = x.shape
    n_t = seq // TM_OUT

    def nat_spec(seg):
        return pl.BlockSpec((None, TM_OUT, SEG_W), lambda b, t: (b, t, seg))

    return pl.pallas_call(
        _out_kernel,
        grid=(bsz, n_t),
        in_specs=[
            pl.BlockSpec((None, TM_OUT, D_MODEL), lambda b, t: (b, t, 0)),
            pl.BlockSpec((None, TM_OUT, A_WIDTH), lambda b, t: (b, t, 0)),
            nat_spec(4), nat_spec(5), nat_spec(6),
            pl.BlockSpec((SGU_GROUPS, CHUNK, CHUNK), lambda b, t: (0, 0, 0)),
            pl.BlockSpec((SGU_GROUPS, CHUNK, SGU_CH), lambda b, t: (0, 0, 0)),
            pl.BlockSpec((D_MODEL, D_MODEL), lambda b, t: (0, 0)),
        ],
        out_specs=pl.BlockSpec((None, TM_OUT, D_MODEL), lambda b, t: (b, t, 0)),
        out_shape=jax.ShapeDtypeStruct((bsz, seq, D_MODEL), F32),
        scratch_shapes=[pltpu.VMEM((TM_OUT, B_WIDTH), BF16)],
        compiler_params=pltpu.CompilerParams(
            dimension_semantics=("arbitrary", "arbitrary"),
            vmem_limit_bytes=VMEM_LIMIT),
        name="out",
    )(x, a_out, nat, nat, nat, w_sp, b_sp, w_out)


def _layer(x, params):
    (norm_w, w_in, q_w, k_w, ln_w, ln_b, p4, p16, tab1, tab4, tab16, w_sp, b_sp, w_out) = params
    nat, r4, r16 = _proj_call(x, norm_w, w_in, q_w, k_w, ln_w, ln_b, p4, p16)
    a_out = _attn_call(nat, r4, r16, tab1, tab4, tab16)
    return _out_call(x, a_out, nat, w_sp, b_sp, w_out)


def kernel(x_prompt, x_sample, rel_bias, norm_w, w_in, q_norm_w, k_norm_w,
           sgu_ln_w, sgu_ln_b, w_spatial, b_spatial, w_out):
    depth = norm_w.shape[0]
    p4 = jnp.asarray(_perm_matrix(4), BF16)
    p16 = jnp.asarray(_perm_matrix(16), BF16)
    tab1 = _bias_tables(rel_bias, 1, KB, (0, HALF, 2 * HALF))
    tab4 = _bias_tables(rel_bias, 4, KB, (0, HALF, 2 * HALF))
    tab16 = _bias_tables(rel_bias, 16, QB, (0,))[0]
    y_prompt, y_sample = x_prompt, x_sample
    for layer in range(depth):
        params = (
            norm_w[layer].reshape(1, D_MODEL),
            w_in[layer].astype(BF16),
            jnp.tile(q_norm_w[layer], N_HEADS).reshape(1, SEG_W),
            jnp.tile(k_norm_w[layer], N_HEADS).reshape(1, SEG_W),
            sgu_ln_w[layer].reshape(1, B_WIDTH),
            sgu_ln_b[layer].reshape(1, B_WIDTH),
            p4, p16, tab1, tab4, tab16,
            w_spatial[layer].astype(BF16),
            jnp.broadcast_to(b_spatial[layer][:, :, None], (SGU_GROUPS, CHUNK, SGU_CH)),
            w_out[layer].astype(BF16),
        )
        y_prompt = _layer(y_prompt, params)
        y_sample = _layer(y_sample, params)
    return (y_prompt, y_sample)
```
